```python
import jax
import jax.numpy as jnp
from jax import lax
import numpy as np

D_MODEL = 2048
BATCH = 8
SEQ = 2048
DEPTH = 1

CHUNK = 64
EPS = 1e-6
HALF_STEP = 0.5
D_FF = 5632
CONV_DIM = 1024
CONV_WIDTH = 31
RET_HEADS = 4
RET_QK_DIM = 256
RET_V_DIM = 512
RET_QK = RET_HEADS * RET_QK_DIM
RET_V = RET_HEADS * RET_V_DIM
ROPE_BASE = 10000.0
N_MOD_LAYER = 9
IN_COLS = 2 * CONV_DIM + 2 * RET_QK + 2 * RET_V + 2 * D_MODEL

kernel_name = "chunk_causal_conv_retention_hybrid"


def _split_cols(u, sizes):
    idx = np.cumsum(sizes)[:-1].tolist()
    return jnp.split(u, idx, axis=-1)


def rms_norm(x, g):
    xf = x.astype(jnp.float32)
    y = xf * lax.rsqrt(jnp.mean(xf * xf, axis=-1, keepdims=True) + EPS)
    return (y * g.astype(jnp.float32)).astype(x.dtype)


def layer_norm(x, g, b):
    xf = x.astype(jnp.float32)
    mu = jnp.mean(xf, axis=-1, keepdims=True)
    var = jnp.mean(jnp.square(xf - mu), axis=-1, keepdims=True)
    y = (xf - mu) * lax.rsqrt(var + EPS)
    return (y * g.astype(jnp.float32) + b.astype(jnp.float32)).astype(x.dtype)


def modulate(h, shift, scale):
    return h * (1.0 + scale[:, None, :]) + shift[:, None, :]


def swiglu(h, w1, w3, w2):
    return (jax.nn.silu(h @ w1) * (h @ w3)) @ w2


def conformer_conv(u, dw_w, dw_b, ln_g, ln_b, pw_w):
    a, b = jnp.split(u, 2, axis=-1)
    y = a * jax.nn.sigmoid(b)
    y = lax.conv_general_dilated(
        y, dw_w[:, None, :].astype(y.dtype), window_strides=(1,),
        padding=[(CONV_WIDTH - 1, 0)],
        dimension_numbers=('NWC', 'WIO', 'NWC'),
        feature_group_count=CONV_DIM) + dw_b
    y = jax.nn.silu(layer_norm(y, ln_g, ln_b))
    return y @ pw_w


def rotary(x, cos, sin):
    x1, x2 = jnp.split(x, 2, axis=-1)
    return jnp.concatenate([x1 * cos - x2 * sin, x1 * sin + x2 * cos], axis=-1)


def retention(q, k, v, g, gn_g, gn_b, w_o):
    B, S = q.shape[0], q.shape[1]
    n_chunks = S // CHUNK
    dt = q.dtype
    q = q.reshape(B, S, RET_HEADS, RET_QK_DIM)
    k = k.reshape(B, S, RET_HEADS, RET_QK_DIM)
    v = v.reshape(B, S, RET_HEADS, RET_V_DIM)
    pos = jnp.arange(S, dtype=jnp.float32)
    inv_freq = ROPE_BASE ** (-jnp.arange(0, RET_QK_DIM, 2, dtype=jnp.float32) / RET_QK_DIM)
    ang = pos[:, None] * inv_freq[None, :]
    cos = jnp.cos(ang)[:, None, :].astype(dt)
    sin = jnp.sin(ang)[:, None, :].astype(dt)
    q = rotary(q, cos, sin) * (RET_QK_DIM ** -0.5)
    k = rotary(k, cos, sin)
    log_gamma = jnp.log(1.0 - 2.0 ** (-5.0 - jnp.arange(RET_HEADS, dtype=jnp.float32)))
    idx = jnp.arange(CHUNK, dtype=jnp.float32)
    dist = jnp.abs(idx[:, None] - idx[None, :])
    intra = jnp.exp(dist[None] * log_gamma[:, None, None]).astype(dt)
    q_dec = jnp.exp((idx + 1.0)[:, None] * log_gamma[None, :]).astype(dt)
    k_dec = jnp.exp((CHUNK - 1.0 - idx)[:, None] * log_gamma[None, :]).astype(dt)
    chunk_dec = jnp.exp(CHUNK * log_gamma).astype(dt)
    qc = q.reshape(B, n_chunks, CHUNK, RET_HEADS, RET_QK_DIM)
    kc = k.reshape(B, n_chunks, CHUNK, RET_HEADS, RET_QK_DIM)
    vc = v.reshape(B, n_chunks, CHUNK, RET_HEADS, RET_V_DIM)
    scores = jnp.einsum('bnchd,bnmhd->bnhcm', qc, kc) * intra[None, None]
    o_intra = jnp.einsum('bnhcm,bnmhe->bnche', scores, vc)
    def step(R, inp):
        qn, kn, vn = inp
        o = jnp.einsum('bchd,bhde->bche', qn * q_dec[None, :, :, None], R)
        R = R * chunk_dec[None, :, None, None] + jnp.einsum(
            'bchd,bche->bhde', kn * k_dec[None, :, :, None], vn)
        return R, o
    R0 = jnp.zeros((B, RET_HEADS, RET_QK_DIM, RET_V_DIM), dtype=v.dtype)
    xs = (jnp.moveaxis(qc, 1, 0), jnp.moveaxis(kc, 1, 0), jnp.moveaxis(vc, 1, 0))
    _, o_cross = lax.scan(step, R0, xs)
    o = (o_intra + jnp.moveaxis(o_cross, 0, 1)).reshape(B, S, RET_HEADS, RET_V_DIM)
    of = o.astype(jnp.float32)
    mu = jnp.mean(of, axis=-1, keepdims=True)
    var = jnp.mean(jnp.square(of - mu), axis=-1, keepdims=True)
    of = ((of - mu) * lax.rsqrt(var + EPS)).reshape(B, S, RET_V)
    o = (of * gn_g.astype(jnp.float32) + gn_b.astype(jnp.float32)).astype(dt)
    return (jax.nn.silu(g) * o) @ w_o


def hybrid_mixer(h, w_in, dw_w, dw_b, ln_g, ln_b, pw_w, gn_g, gn_b, ret_w_o, w_out):
    u = h @ w_in
    u_conv, q, k, v, g, gate_c, gate_r = _split_cols(
        u, [2 * CONV_DIM, RET_QK, RET_QK, RET_V, RET_V, D_MODEL, D_MODEL])
    y_conv = conformer_conv(u_conv, dw_w, dw_b, ln_g, ln_b, pw_w)
    y_ret = retention(q, k, v, g, gn_g, gn_b, ret_w_o)
    merged = jax.nn.sigmoid(gate_c) * y_conv + jax.nn.sigmoid(gate_r) * y_ret
    return merged @ w_out


def setup_inputs(seed: int = 0) -> dict:
    key = jax.random.key(seed)
    ks = iter(jax.random.split(key, 32))
    L, D = DEPTH, D_MODEL

    def w(shape, fan_in):
        return jax.random.normal(next(ks), shape, jnp.float32) * (fan_in ** -0.5)

    def gain(shape):
        return 1.0 + 0.05 * jax.random.normal(next(ks), shape, jnp.float32)

    def bias(shape):
        return 0.02 * jax.random.normal(next(ks), shape, jnp.float32)

    return {
        "x": jax.random.normal(next(ks), (BATCH, SEQ, D), jnp.float32),
        "c": jax.random.normal(next(ks), (BATCH, D), jnp.float32),
        "ada_w": w((L, D, N_MOD_LAYER * D), D),
        "ada_b": bias((L, N_MOD_LAYER * D)),
        "ffn1_norm": gain((L, D)),
        "ffn1_w1": w((L, D, D_FF), D),
        "ffn1_w3": w((L, D, D_FF), D),
        "ffn1_w2": w((L, D_FF, D), D_FF),
        "mix_norm": gain((L, D)),
        "w_in": w((L, D, IN_COLS), D),
        "conv_dw_w": w((L, CONV_WIDTH, CONV_DIM), CONV_WIDTH),
        "conv_dw_b": bias((L, CONV_DIM)),
        "conv_ln_g": gain((L, CONV_DIM)),
        "conv_ln_b": bias((L, CONV_DIM)),
        "conv_pw_w": w((L, CONV_DIM, D), CONV_DIM),
        "ret_gn_g": gain((L, RET_V)),
        "ret_gn_b": bias((L, RET_V)),
        "ret_w_o": w((L, RET_V, D), RET_V),
        "w_out": w((L, D, D), D),
        "ffn2_norm": gain((L, D)),
        "ffn2_w1": w((L, D, D_FF), D),
        "ffn2_w3": w((L, D, D_FF), D),
        "ffn2_w2": w((L, D_FF, D), D_FF),
        "final_norm": gain((D,)),
        "ada_f_w": w((D, 2 * D), D),
        "ada_f_b": bias((2 * D,)),
    }


def reference(x, c, ada_w, ada_b, ffn1_norm, ffn1_w1, ffn1_w3, ffn1_w2, mix_norm, w_in,
              conv_dw_w, conv_dw_b, conv_ln_g, conv_ln_b, conv_pw_w, ret_gn_g, ret_gn_b,
              ret_w_o, w_out, ffn2_norm, ffn2_w1, ffn2_w3, ffn2_w2, final_norm,
              ada_f_w, ada_f_b):
    c_act = jax.nn.silu(c)
    for l in range(DEPTH):
        mod = c_act @ ada_w[l] + ada_b[l]
        sh1, sc1, g1, sh2, sc2, g2, sh3, sc3, g3 = jnp.split(mod, N_MOD_LAYER, axis=-1)
        h = modulate(rms_norm(x, ffn1_norm[l]), sh1, sc1)
        x = x + HALF_STEP * g1[:, None, :] * swiglu(h, ffn1_w1[l], ffn1_w3[l], ffn1_w2[l])
        h = modulate(rms_norm(x, mix_norm[l]), sh2, sc2)
        x = x + g2[:, None, :] * hybrid_mixer(
            h, w_in[l], conv_dw_w[l], conv_dw_b[l], conv_ln_g[l], conv_ln_b[l],
            conv_pw_w[l], ret_gn_g[l], ret_gn_b[l], ret_w_o[l], w_out[l])
        h = modulate(rms_norm(x, ffn2_norm[l]), sh3, sc3)
        x = x + HALF_STEP * g3[:, None, :] * swiglu(h, ffn2_w1[l], ffn2_w3[l], ffn2_w2[l])
    f = c_act @ ada_f_w + ada_f_b
    sh_f, sc_f = jnp.split(f, 2, axis=-1)
    return modulate(rms_norm(x, final_norm), sh_f, sc_f)
```

```python
import functools

import numpy as np
import jax
import jax.numpy as jnp
from jax import lax
from jax.experimental import pallas as pl
from jax.experimental.pallas import tpu as pltpu

D_MODEL = 2048
D_FF = 5632
CHUNK = 64
EPS = 1e-6
HALF_STEP = 0.5
CONV_DIM = 1024
CONV_WIDTH = 31
RET_HEADS = 4
RET_QK_DIM = 256
RET_V_DIM = 512
RET_QK = RET_HEADS * RET_QK_DIM
RET_V = RET_HEADS * RET_V_DIM
ROPE_BASE = 10000.0
N_MOD = 9
IN_COLS = 2 * CONV_DIM + 2 * RET_QK + 2 * RET_V + 2 * D_MODEL

COL_A = 0
COL_B = CONV_DIM
COL_Q = 2 * CONV_DIM
COL_K = COL_Q + RET_QK
COL_V = COL_K + RET_QK
COL_G = COL_V + RET_V
COL_GC = COL_G + RET_V
COL_GR = COL_GC + D_MODEL

V7X_VMEM_BYTES = 64 * 1024 * 1024
VMEM_LIMIT = 56 * 1024 * 1024

BF16 = jnp.bfloat16
F32 = jnp.float32

CONV_HALO = 32
RET_TILE = 256


def _params(sem):
    return pltpu.CompilerParams(dimension_semantics=sem, vmem_limit_bytes=VMEM_LIMIT)


def _rms_mod(x, gain, shift, scale):
    ms = jnp.mean(x * x, axis=-1, keepdims=True)
    y = x * lax.rsqrt(ms + EPS) * gain
    return y * (1.0 + scale) + shift


def _mod_spec(idx, tiles_per_batch):
    return pl.BlockSpec((None, None, 1, D_MODEL),
                        lambda i, j: (i // tiles_per_batch, idx, 0, 0))


def _adaln_kernel(c_ref, w_ref, b_ref, o_ref):
    c = c_ref[...]
    ca = (c * jax.nn.sigmoid(c)).astype(BF16)
    o_ref[...] = jnp.dot(ca, w_ref[...].astype(BF16),
                         preferred_element_type=F32) + b_ref[...]


def _adaln(c, w, b, tn=1024):
    bsz, d = c.shape
    n = w.shape[1]
    return pl.pallas_call(
        _adaln_kernel,
        out_shape=jax.ShapeDtypeStruct((bsz, n), F32),
        grid=(n // tn,),
        in_specs=[pl.BlockSpec((bsz, d), lambda j: (0, 0)),
                  pl.BlockSpec((d, tn), lambda j: (0, j)),
                  pl.BlockSpec((1, tn), lambda j: (0, j))],
        out_specs=pl.BlockSpec((bsz, tn), lambda j: (0, j)),
        compiler_params=_params(("arbitrary",)),
        name="adaln",
    )(c, w, b.reshape(1, n))


def _ffn_kernel(*refs, final):
    if final:
        (x_ref, gain_ref, sh_ref, sc_ref, gt_ref, w1_ref, w3_ref, w2_ref,
         fg_ref, fsh_ref, fsc_ref, o_ref, h_ref) = refs
    else:
        (x_ref, gain_ref, sh_ref, sc_ref, gt_ref, w1_ref, w3_ref, w2_ref,
         o_ref, h_ref) = refs
    j = pl.program_id(1)

    @pl.when(j == 0)
    def _():
        h = _rms_mod(x_ref[...], gain_ref[...], sh_ref[...], sc_ref[...])
        h_ref[...] = h.astype(BF16)
        o_ref[...] = jnp.zeros_like(o_ref)

    h = h_ref[...]
    a = jnp.dot(h, w1_ref[...], preferred_element_type=F32)
    b = jnp.dot(h, w3_ref[...], preferred_element_type=F32)
    p = (a * jax.nn.sigmoid(a) * b).astype(BF16)
    o_ref[...] += jnp.dot(p, w2_ref[...], preferred_element_type=F32)

    @pl.when(j == pl.num_programs(1) - 1)
    def _():
        xn = x_ref[...] + HALF_STEP * gt_ref[...] * o_ref[...]
        if final:
            xn = _rms_mod(xn, fg_ref[...], fsh_ref[...], fsc_ref[...])
        o_ref[...] = xn


def _ffn(x2, gain, mod4, mod_idx, w1, w3, w2, seq, final=None, tm=512, tf=512):
    n_tok, d = x2.shape
    tpb = seq // tm
    row = pl.BlockSpec((tm, d), lambda i, j: (i, 0))
    vec = pl.BlockSpec((1, d), lambda i, j: (0, 0))
    in_specs = [row, vec,
                _mod_spec(mod_idx, tpb), _mod_spec(mod_idx + 1, tpb),
                _mod_spec(mod_idx + 2, tpb),
                pl.BlockSpec((d, tf), lambda i, j: (0, j)),
                pl.BlockSpec((d, tf), lambda i, j: (0, j)),
                pl.BlockSpec((tf, d), lambda i, j: (j, 0))]
    args = [x2, gain.reshape(1, d), mod4, mod4, mod4, w1, w3, w2]
    if final is not None:
        fgain, f4 = final
        in_specs += [vec, _mod_spec(0, tpb), _mod_spec(1, tpb)]
        args += [fgain.reshape(1, d), f4, f4]
    return pl.pallas_call(
        functools.partial(_ffn_kernel, final=final is not None),
        out_shape=jax.ShapeDtypeStruct((n_tok, d), F32),
        grid=(n_tok // tm, D_FF // tf),
        in_specs=in_specs,
        out_specs=row,
        scratch_shapes=[pltpu.VMEM((tm, d), BF16)],
        compiler_params=_params(("arbitrary", "arbitrary")),
        name="ffn_final" if final is not None else "ffn",
    )(*args)


def _inproj_kernel(x_ref, gain_ref, sh_ref, sc_ref, w_ref, o_ref, h_ref):
    @pl.when(pl.program_id(1) == 0)
    def _():
        h = _rms_mod(x_ref[...], gain_ref[...], sh_ref[...], sc_ref[...])
        h_ref[...] = h.astype(BF16)

    o_ref[...] = jnp.dot(h_ref[...], w_ref[...],
                         preferred_element_type=F32).astype(BF16)


def _in_proj(x2, gain, mod4, w_in, seq, tm=1024, tn=1024):
    n_tok, d = x2.shape
    tpb = seq // tm
    return pl.pallas_call(
        _inproj_kernel,
        out_shape=jax.ShapeDtypeStruct((n_tok, IN_COLS), BF16),
        grid=(n_tok // tm, IN_COLS // tn),
        in_specs=[pl.BlockSpec((tm, d), lambda i, j: (i, 0)),
                  pl.BlockSpec((1, d), lambda i, j: (0, 0)),
                  _mod_spec(3, tpb), _mod_spec(4, tpb),
                  pl.BlockSpec((d, tn), lambda i, j: (0, j))],
        out_specs=pl.BlockSpec((tm, tn), lambda i, j: (i, j)),
        scratch_shapes=[pltpu.VMEM((tm, d), BF16)],
        compiler_params=_params(("arbitrary", "arbitrary")),
        name="in_proj",
    )(x2, gain.reshape(1, d), mod4, mod4, w_in)


def _conv_kernel(a_ref, b_ref, w_ref, wb_ref, lg_ref, lb_ref, o_ref, ybuf, cbuf, *, ts):
    t = pl.program_id(1)

    @pl.when(t == 0)
    def _():
        ybuf[0:CONV_HALO, :] = jnp.zeros((CONV_HALO, CONV_DIM), F32)

    a = a_ref[...].astype(F32)
    b = b_ref[...].astype(F32)
    ybuf[CONV_HALO:CONV_HALO + ts, :] = a * jax.nn.sigmoid(b)

    lead = CONV_HALO - (CONV_WIDTH - 1)
    win = 8 + CONV_HALO

    def body(r, carry):
        base = pl.multiple_of(r * 8, 8)
        window = ybuf[pl.ds(base, win), :]
        acc = jnp.broadcast_to(wb_ref[...], (8, CONV_DIM))
        for k in range(CONV_WIDTH):
            acc = acc + w_ref[k:k + 1, :] * window[lead + k:lead + k + 8, :]
        cbuf[pl.ds(base, 8), :] = acc
        return carry

    lax.fori_loop(0, ts // 8, body, 0)

    ybuf[0:CONV_HALO, :] = ybuf[ts:ts + CONV_HALO, :]

    y = cbuf[...]
    mu = jnp.mean(y, axis=-1, keepdims=True)
    yc = y - mu
    var = jnp.mean(yc * yc, axis=-1, keepdims=True)
    z = yc * lax.rsqrt(var + EPS) * lg_ref[...] + lb_ref[...]
    o_ref[...] = (z * jax.nn.sigmoid(z)).astype(BF16)


def _conv_branch(u, dw_w, dw_b, ln_g, ln_b, bsz, seq, ts=512):
    n_tok = u.shape[0]
    tpb = seq // ts
    cb = CONV_DIM
    vec = pl.BlockSpec((1, cb), lambda b, t: (0, 0))
    return pl.pallas_call(
        functools.partial(_conv_kernel, ts=ts),
        out_shape=jax.ShapeDtypeStruct((n_tok, cb), BF16),
        grid=(bsz, tpb),
        in_specs=[pl.BlockSpec((ts, cb), lambda b, t: (b * tpb + t, COL_A // cb)),
                  pl.BlockSpec((ts, cb), lambda b, t: (b * tpb + t, COL_B // cb)),
                  pl.BlockSpec((CONV_WIDTH, cb), lambda b, t: (0, 0)),
                  vec, vec, vec],
        out_specs=pl.BlockSpec((ts, cb), lambda b, t: (b * tpb + t, 0)),
        scratch_shapes=[pltpu.VMEM((ts + CONV_HALO + 8, cb), F32),
                        pltpu.VMEM((ts, cb), F32)],
        compiler_params=_params(("arbitrary", "arbitrary")),
        name="conv_branch",
    )(u, u, dw_w, dw_b.reshape(1, cb), ln_g.reshape(1, cb), ln_b.reshape(1, cb))


def _rotary(xh, cos, sin):
    half = RET_QK_DIM // 2
    x1 = xh[:, :half]
    x2 = xh[:, half:]
    return jnp.concatenate([x1 * cos - x2 * sin, x1 * sin + x2 * cos], axis=-1)


def _ret_kernel(q_ref, k_ref, v_ref, g_ref, cos_ref, sin_ref, dmask_ref, qdec_ref,
                kdec_ref, gg_ref, gb_ref, o_ref, state_ref, *, tile_decay):
    @pl.when(pl.program_id(1) == 0)
    def _():
        state_ref[...] = jnp.zeros_like(state_ref)

    cos = cos_ref[...]
    sin = sin_ref[...]
    for h in range(RET_HEADS):
        qs = slice(h * RET_QK_DIM, (h + 1) * RET_QK_DIM)
        vs = slice(h * RET_V_DIM, (h + 1) * RET_V_DIM)
        q = _rotary(q_ref[:, qs].astype(F32), cos, sin) * (RET_QK_DIM ** -0.5)
        k = _rotary(k_ref[:, qs].astype(F32), cos, sin)
        v = v_ref[:, vs]
        scores = lax.dot_general(q.astype(BF16), k.astype(BF16),
                                 (((1,), (1,)), ((), ())),
                                 preferred_element_type=F32) * dmask_ref[h]
        o = jnp.dot(scores.astype(BF16), v, preferred_element_type=F32)
        state = state_ref[h]
        o = o + jnp.dot((q * qdec_ref[h]).astype(BF16), state.astype(BF16),
                        preferred_element_type=F32)
        kv = lax.dot_general((k * kdec_ref[h]).astype(BF16), v,
                             (((0,), (0,)), ((), ())),
                             preferred_element_type=F32)
        state_ref[h] = state * tile_decay[h] + kv
        mu = jnp.mean(o, axis=-1, keepdims=True)
        oc = o - mu
        var = jnp.mean(oc * oc, axis=-1, keepdims=True)
        on = oc * lax.rsqrt(var + EPS) * gg_ref[:, vs] + gb_ref[:, vs]
        g = g_ref[:, vs].astype(F32)
        o_ref[:, vs] = (g * jax.nn.sigmoid(g) * on).astype(BF16)


def _retention_consts(seq):
    t = RET_TILE
    pos = jnp.arange(seq, dtype=F32)
    inv_freq = ROPE_BASE ** (-jnp.arange(0, RET_QK_DIM, 2, dtype=F32) / RET_QK_DIM)
    ang = pos[:, None] * inv_freq[None, :]
    cos = jnp.cos(ang)
    sin = jnp.sin(ang)
    log_gamma = jnp.log(1.0 - 2.0 ** (-5.0 - jnp.arange(RET_HEADS, dtype=F32)))
    idx = jnp.arange(t, dtype=F32)
    diff = idx[:, None] - idx[None, :]
    chunk_i = jnp.arange(t)[:, None] // CHUNK
    chunk_j = jnp.arange(t)[None, :] // CHUNK
    dmask = jnp.where((chunk_j <= chunk_i)[None],
                      jnp.exp(jnp.abs(diff)[None] * log_gamma[:, None, None]), 0.0)
    qdec = jnp.exp((idx + 1.0)[None, :, None] * log_gamma[:, None, None])
    kdec = jnp.exp((t - 1.0 - idx)[None, :, None] * log_gamma[:, None, None])
    return cos, sin, dmask, qdec, kdec


def _tile_decay():
    gamma = 1.0 - 2.0 ** (-5.0 - np.arange(RET_HEADS, dtype=np.float64))
    return tuple(float(x) for x in gamma ** RET_TILE)


def _retention(u, gn_g, gn_b, bsz, seq):
    n_tok = u.shape[0]
    t = RET_TILE
    tpb = seq // t
    cos, sin, dmask, qdec, kdec = _retention_consts(seq)
    half = RET_QK_DIM // 2

    def col(width, off):
        return pl.BlockSpec((t, width), lambda b, i: (b * tpb + i, off // width))

    const3 = lambda shape: pl.BlockSpec(shape, lambda b, i: (0, 0, 0))
    vec = pl.BlockSpec((1, RET_V), lambda b, i: (0, 0))
    return pl.pallas_call(
        functools.partial(_ret_kernel, tile_decay=_tile_decay()),
        out_shape=jax.ShapeDtypeStruct((n_tok, RET_V), BF16),
        grid=(bsz, tpb),
        in_specs=[col(RET_QK, COL_Q), col(RET_QK, COL_K), col(RET_V, COL_V),
                  col(RET_V, COL_G),
                  pl.BlockSpec((t, half), lambda b, i: (i, 0)),
                  pl.BlockSpec((t, half), lambda b, i: (i, 0)),
                  const3((RET_HEADS, t, t)), const3((RET_HEADS, t, 1)),
                  const3((RET_HEADS, t, 1)), vec, vec],
        out_specs=pl.BlockSpec((t, RET_V), lambda b, i: (b * tpb + i, 0)),
        scratch_shapes=[pltpu.VMEM((RET_HEADS, RET_QK_DIM, RET_V_DIM), F32)],
        compiler_params=_params(("arbitrary", "arbitrary")),
        name="retention",
    )(u, u, u, u, cos, sin, dmask, qdec, kdec,
      gn_g.reshape(1, RET_V), gn_b.reshape(1, RET_V))


def _merge_kernel(x_ref, yc_ref, yr_ref, gc_ref, gr_ref, gt_ref, pw_ref, wo_ref,
                  wout_ref, o_ref):
    j = pl.program_id(1)

    @pl.when(j == 0)
    def _():
        o_ref[...] = jnp.zeros_like(o_ref)

    yc = jnp.dot(yc_ref[...], pw_ref[...], preferred_element_type=F32)
    yr = jnp.dot(yr_ref[...], wo_ref[...], preferred_element_type=F32)
    m = (jax.nn.sigmoid(gc_ref[...].astype(F32)) * yc
         + jax.nn.sigmoid(gr_ref[...].astype(F32)) * yr)
    o_ref[...] += jnp.dot(m.astype(BF16), wout_ref[...], preferred_element_type=F32)

    @pl.when(j == pl.num_programs(1) - 1)
    def _():
        o_ref[...] = x_ref[...] + gt_ref[...] * o_ref[...]


def _merge(x2, ycf, yrf, u, mod4, pw_w, w_o, w_out, seq, tm=512, tn=512):
    n_tok, d = x2.shape
    tpb = seq // tm
    row = pl.BlockSpec((tm, d), lambda i, j: (i, 0))
    return pl.pallas_call(
        _merge_kernel,
        out_shape=jax.ShapeDtypeStruct((n_tok, d), F32),
        grid=(n_tok // tm, d // tn),
        in_specs=[row,
                  pl.BlockSpec((tm, CONV_DIM), lambda i, j: (i, 0)),
                  pl.BlockSpec((tm, RET_V), lambda i, j: (i, 0)),
                  pl.BlockSpec((tm, tn), lambda i, j: (i, COL_GC // tn + j)),
                  pl.BlockSpec((tm, tn), lambda i, j: (i, COL_GR // tn + j)),
                  _mod_spec(5, tpb),
                  pl.BlockSpec((CONV_DIM, tn), lambda i, j: (0, j)),
                  pl.BlockSpec((RET_V, tn), lambda i, j: (0, j)),
                  pl.BlockSpec((tn, d), lambda i, j: (j, 0))],
        out_specs=row,
        compiler_params=_params(("arbitrary", "arbitrary")),
        name="merge",
    )(x2, ycf, yrf, u, u, mod4, pw_w, w_o, w_out)


def kernel(x, c, ada_w, ada_b, ffn1_norm, ffn1_w1, ffn1_w3, ffn1_w2, mix_norm, w_in,
           conv_dw_w, conv_dw_b, conv_ln_g, conv_ln_b, conv_pw_w, ret_gn_g, ret_gn_b,
           ret_w_o, w_out, ffn2_norm, ffn2_w1, ffn2_w3, ffn2_w2, final_norm,
           ada_f_w, ada_f_b):
    bsz, seq, d = x.shape
    depth = ada_w.shape[0]
    x2 = x.reshape(bsz * seq, d)
    f4 = _adaln(c, ada_f_w, ada_f_b).reshape(bsz, 2, 1, d)
    for l in range(depth):
        last = l == depth - 1
        mod4 = _adaln(c, ada_w[l], ada_b[l]).reshape(bsz, N_MOD, 1, d)
        x2 = _ffn(x2, ffn1_norm[l], mod4, 0, ffn1_w1[l].astype(BF16),
                  ffn1_w3[l].astype(BF16), ffn1_w2[l].astype(BF16), seq)
        u = _in_proj(x2, mix_norm[l], mod4, w_in[l].astype(BF16), seq)
        ycf = _conv_branch(u, conv_dw_w[l], conv_dw_b[l], conv_ln_g[l], conv_ln_b[l],
                           bsz, seq)
        yrf = _retention(u, ret_gn_g[l], ret_gn_b[l], bsz, seq)
        x2 = _merge(x2, ycf, yrf, u, mod4, conv_pw_w[l].astype(BF16),
                    ret_w_o[l].astype(BF16), w_out[l].astype(BF16), seq)
        x2 = _ffn(x2, ffn2_norm[l], mod4, 6, ffn2_w1[l].astype(BF16),
                  ffn2_w3[l].astype(BF16), ffn2_w2[l].astype(BF16), seq,
                  final=(final_norm, f4) if last else None)
    if depth == 0:
        raise NotImplementedError("kernel expects at least one layer")
    return x2.reshape(bsz, seq, d)
```

```python
import functools

import numpy as np
import jax
import jax.numpy as jnp
from jax import lax
from jax.experimental import pallas as pl
from jax.experimental.pallas import tpu as pltpu

D_MODEL = 2048
D_FF = 5632
CHUNK = 64
EPS = 1e-6
HALF_STEP = 0.5
CONV_DIM = 1024
CONV_WIDTH = 31
RET_HEADS = 4
RET_QK_DIM = 256
RET_V_DIM = 512
RET_QK = RET_HEADS * RET_QK_DIM
RET_V = RET_HEADS * RET_V_DIM
ROPE_BASE = 10000.0
N_MOD = 9
IN_COLS = 2 * CONV_DIM + 2 * RET_QK + 2 * RET_V + 2 * D_MODEL

COL_A = 0
COL_B = CONV_DIM
COL_Q = 2 * CONV_DIM
COL_K = COL_Q + RET_QK
COL_V = COL_K + RET_QK
COL_G = COL_V + RET_V
COL_GC = COL_G + RET_V
COL_GR = COL_GC + D_MODEL

V7X_VMEM_BYTES = 64 * 1024 * 1024
VMEM_LIMIT = 56 * 1024 * 1024

BF16 = jnp.bfloat16
F32 = jnp.float32

SUBLANES = 8
CONV_HALO = 32
CONV_ROW_BLOCK = 32
CONV_COL_BLOCK = 512
RET_TILE = 256


def _params(sem):
    return pltpu.CompilerParams(dimension_semantics=sem, vmem_limit_bytes=VMEM_LIMIT)


def _rms_mod(x, gain, shift, scale):
    ms = jnp.mean(x * x, axis=-1, keepdims=True)
    y = x * lax.rsqrt(ms + EPS) * gain
    return y * (1.0 + scale) + shift


def _mod_spec(idx, tiles_per_batch):
    return pl.BlockSpec((None, None, 1, D_MODEL),
                        lambda i, j: (i // tiles_per_batch, idx, 0, 0))


def _adaln_kernel(c_ref, w_ref, b_ref, o_ref):
    c = c_ref[...]
    ca = (c * jax.nn.sigmoid(c)).astype(BF16)
    o_ref[...] = jnp.dot(ca, w_ref[...].astype(BF16),
                         preferred_element_type=F32) + b_ref[...]


def _adaln(c, w, b, tn=1024):
    bsz, d = c.shape
    n = w.shape[1]
    return pl.pallas_call(
        _adaln_kernel,
        out_shape=jax.ShapeDtypeStruct((bsz, n), F32),
        grid=(n // tn,),
        in_specs=[pl.BlockSpec((bsz, d), lambda j: (0, 0)),
                  pl.BlockSpec((d, tn), lambda j: (0, j)),
                  pl.BlockSpec((1, tn), lambda j: (0, j))],
        out_specs=pl.BlockSpec((bsz, tn), lambda j: (0, j)),
        compiler_params=_params(("arbitrary",)),
        name="adaln",
    )(c, w, b.reshape(1, n))


def _prep_next_h(xc_ref, gain_ref, sh_ref, sc_ref, h_ref, slot):
    rows = xc_ref.shape[0]
    n_chunks = h_ref.shape[1] // rows
    c = jnp.minimum(pl.program_id(1), n_chunks - 1)
    r0 = pl.multiple_of(c * rows, rows)
    h = _rms_mod(xc_ref[...], gain_ref[...], sh_ref[...], sc_ref[...])
    h_ref[slot, pl.ds(r0, rows), :] = h.astype(BF16)


def _next_chunk_spec(rows, d, tm, n_tiles):
    cpt = tm // rows
    return pl.BlockSpec(
        (None, rows, d),
        lambda i, j: (jnp.minimum(i + 1, n_tiles - 1) * cpt + jnp.minimum(j, cpt - 1), 0, 0))


def _next_mod_spec(idx, tiles_per_batch, n_tiles):
    return pl.BlockSpec(
        (None, None, 1, D_MODEL),
        lambda i, j: (jnp.minimum(i + 1, n_tiles - 1) // tiles_per_batch, idx, 0, 0))


def _ffn_kernel(*refs, final):
    if final:
        (x_ref, xc_ref, gain_ref, sh_ref, sc_ref, shn_ref, scn_ref, gt_ref, w1_ref, w3_ref,
         w2_ref, fg_ref, fsh_ref, fsc_ref, o_ref, h_ref) = refs
    else:
        (x_ref, xc_ref, gain_ref, sh_ref, sc_ref, shn_ref, scn_ref, gt_ref, w1_ref, w3_ref,
         w2_ref, o_ref, h_ref) = refs
    i = pl.program_id(0)
    j = pl.program_id(1)
    slot = i % 2

    @pl.when((i == 0) & (j == 0))
    def _():
        h = _rms_mod(x_ref[...], gain_ref[...], sh_ref[...], sc_ref[...])
        h_ref[0] = h.astype(BF16)

    @pl.when(j == 0)
    def _():
        o_ref[...] = jnp.zeros_like(o_ref)

    h = h_ref[slot]
    a = jnp.dot(h, w1_ref[...], preferred_element_type=F32)
    b = jnp.dot(h, w3_ref[...], preferred_element_type=F32)
    p = (a * jax.nn.sigmoid(a) * b).astype(BF16)
    o_ref[...] += jnp.dot(p, w2_ref[...], preferred_element_type=F32)
    _prep_next_h(xc_ref, gain_ref, shn_ref, scn_ref, h_ref, 1 - slot)

    @pl.when(j == pl.num_programs(1) - 1)
    def _():
        xn = x_ref[...] + HALF_STEP * gt_ref[...] * o_ref[...]
        if final:
            xn = _rms_mod(xn, fg_ref[...], fsh_ref[...], fsc_ref[...])
        o_ref[...] = xn


def _ffn(x2, gain, mod4, mod_idx, w1, w3, w2, seq, final=None, tm=512, tf=512,
         prep_rows=64):
    n_tok, d = x2.shape
    tpb = seq // tm
    n_tiles = n_tok // tm
    row = pl.BlockSpec((tm, d), lambda i, j: (i, 0))
    vec = pl.BlockSpec((1, d), lambda i, j: (0, 0))
    in_specs = [row, _next_chunk_spec(prep_rows, d, tm, n_tiles), vec,
                _mod_spec(mod_idx, tpb), _mod_spec(mod_idx + 1, tpb),
                _next_mod_spec(mod_idx, tpb, n_tiles), _next_mod_spec(mod_idx + 1, tpb, n_tiles),
                _mod_spec(mod_idx + 2, tpb),
                pl.BlockSpec((d, tf), lambda i, j: (0, j)),
                pl.BlockSpec((d, tf), lambda i, j: (0, j)),
                pl.BlockSpec((tf, d), lambda i, j: (j, 0))]
    args = [x2, x2.reshape(n_tok // prep_rows, prep_rows, d), gain.reshape(1, d),
            mod4, mod4, mod4, mod4, mod4, w1, w3, w2]
    if final is not None:
        fgain, f4 = final
        in_specs += [vec, _mod_spec(0, tpb), _mod_spec(1, tpb)]
        args += [fgain.reshape(1, d), f4, f4]
    return pl.pallas_call(
        functools.partial(_ffn_kernel, final=final is not None),
        out_shape=jax.ShapeDtypeStruct((n_tok, d), F32),
        grid=(n_tiles, D_FF // tf),
        in_specs=in_specs,
        out_specs=row,
        scratch_shapes=[pltpu.VMEM((2, tm, d), BF16)],
        compiler_params=_params(("arbitrary", "arbitrary")),
        name="ffn_final" if final is not None else "ffn",
    )(*args)


def _inproj_kernel(x0_ref, xc_ref, gain_ref, sh_ref, sc_ref, shn_ref, scn_ref, w_ref, o_ref,
                   h_ref):
    i = pl.program_id(0)
    slot = i % 2

    @pl.when((i == 0) & (pl.program_id(1) == 0))
    def _():
        h = _rms_mod(x0_ref[...], gain_ref[...], sh_ref[...], sc_ref[...])
        h_ref[0] = h.astype(BF16)

    o_ref[...] = jnp.dot(h_ref[slot], w_ref[...],
                         preferred_element_type=F32).astype(BF16)
    _prep_next_h(xc_ref, gain_ref, shn_ref, scn_ref, h_ref, 1 - slot)


def _in_proj(x2, gain, mod4, w_in, seq, tm=1024, tn=1024, prep_rows=128):
    n_tok, d = x2.shape
    tpb = seq // tm
    n_tiles = n_tok // tm
    return pl.pallas_call(
        _inproj_kernel,
        out_shape=jax.ShapeDtypeStruct((n_tok, IN_COLS), BF16),
        grid=(n_tiles, IN_COLS // tn),
        in_specs=[pl.BlockSpec((tm, d), lambda i, j: (0, 0)),
                  _next_chunk_spec(prep_rows, d, tm, n_tiles),
                  pl.BlockSpec((1, d), lambda i, j: (0, 0)),
                  _mod_spec(3, tpb), _mod_spec(4, tpb),
                  _next_mod_spec(3, tpb, n_tiles), _next_mod_spec(4, tpb, n_tiles),
                  pl.BlockSpec((d, tn), lambda i, j: (0, j))],
        out_specs=pl.BlockSpec((tm, tn), lambda i, j: (i, j)),
        scratch_shapes=[pltpu.VMEM((2, tm, d), BF16)],
        compiler_params=_params(("arbitrary", "arbitrary")),
        name="in_proj",
    )(x2, x2.reshape(n_tok // prep_rows, prep_rows, d), gain.reshape(1, d),
      mod4, mod4, mod4, mod4, w_in)


def _conv_kernel(a_ref, b_ref, w_ref, wb_ref, lg_ref, lb_ref, o_ref, ysh, wbc, cbuf, *, ts):
    t = pl.program_id(1)

    @pl.when(t == 0)
    def _():
        ysh[0, 0:CONV_HALO, :] = jnp.zeros((CONV_HALO, CONV_DIM), F32)
        for k in range(CONV_WIDTH):
            wbc[k] = jnp.broadcast_to(w_ref[k:k + 1, :], (SUBLANES, CONV_DIM))

    a = a_ref[...].astype(F32)
    b = b_ref[...].astype(F32)
    ysh[0, CONV_HALO:CONV_HALO + ts, :] = a * jax.nn.sigmoid(b)

    lead = CONV_HALO - (CONV_WIDTH - 1)
    max_shifted_off = ((lead + CONV_WIDTH - 2) // SUBLANES) * SUBLANES

    def shift_body(g, carry):
        base = pl.multiple_of(g * SUBLANES, SUBLANES)
        window = ysh[0, pl.ds(base, 2 * SUBLANES), :]
        for r in range(1, SUBLANES):
            ysh[r, pl.ds(base, SUBLANES), :] = window[r:r + SUBLANES, :]
        return carry

    lax.fori_loop(0, (ts + max_shifted_off) // SUBLANES, shift_body, 0)

    groups = CONV_ROW_BLOCK // SUBLANES

    def conv_body(i, carry):
        base = pl.multiple_of(i * CONV_ROW_BLOCK, CONV_ROW_BLOCK)
        for c0 in range(0, CONV_DIM, CONV_COL_BLOCK):
            cs = slice(c0, c0 + CONV_COL_BLOCK)
            bias = jnp.broadcast_to(wb_ref[:, cs], (SUBLANES, CONV_COL_BLOCK))
            accs = [bias] * groups
            for k in range(CONV_WIDTH):
                off, r = divmod(lead + k, SUBLANES)
                wk = wbc[k, :, cs]
                for gi in range(groups):
                    rows = pl.ds(base + (off + gi) * SUBLANES, SUBLANES)
                    accs[gi] = accs[gi] + wk * ysh[r, rows, cs]
            for gi in range(groups):
                cbuf[pl.ds(base + gi * SUBLANES, SUBLANES), cs] = accs[gi]
        return carry

    lax.fori_loop(0, ts // CONV_ROW_BLOCK, conv_body, 0)

    ysh[0, 0:CONV_HALO, :] = ysh[0, ts:ts + CONV_HALO, :]

    y = cbuf[...]
    mu = jnp.mean(y, axis=-1, keepdims=True)
    yc = y - mu
    var = jnp.mean(yc * yc, axis=-1, keepdims=True)
    z = yc * lax.rsqrt(var + EPS) * lg_ref[...] + lb_ref[...]
    o_ref[...] = (z * jax.nn.sigmoid(z)).astype(BF16)


def _conv_branch(u, dw_w, dw_b, ln_g, ln_b, bsz, seq, ts=512):
    n_tok = u.shape[0]
    tpb = seq // ts
    cb = CONV_DIM
    vec = pl.BlockSpec((1, cb), lambda b, t: (0, 0))
    return pl.pallas_call(
        functools.partial(_conv_kernel, ts=ts),
        out_shape=jax.ShapeDtypeStruct((n_tok, cb), BF16),
        grid=(bsz, tpb),
        in_specs=[pl.BlockSpec((ts, cb), lambda b, t: (b * tpb + t, COL_A // cb)),
                  pl.BlockSpec((ts, cb), lambda b, t: (b * tpb + t, COL_B // cb)),
                  pl.BlockSpec((CONV_WIDTH, cb), lambda b, t: (0, 0)),
                  vec, vec, vec],
        out_specs=pl.BlockSpec((ts, cb), lambda b, t: (b * tpb + t, 0)),
        scratch_shapes=[pltpu.VMEM((SUBLANES, ts + CONV_HALO, cb), F32),
                        pltpu.VMEM((CONV_WIDTH, SUBLANES, cb), F32),
                        pltpu.VMEM((ts, cb), F32)],
        compiler_params=_params(("arbitrary", "arbitrary")),
        name="conv_branch",
    )(u, u, dw_w, dw_b.reshape(1, cb), ln_g.reshape(1, cb), ln_b.reshape(1, cb))


def _rotary(xh, cos, sin):
    half = RET_QK_DIM // 2
    x1 = xh[:, :half]
    x2 = xh[:, half:]
    return jnp.concatenate([x1 * cos - x2 * sin, x1 * sin + x2 * cos], axis=-1)


def _ret_kernel(q_ref, k_ref, v_ref, g_ref, cos_ref, sin_ref, dmask_ref, qdec_ref,
                kdec_ref, gg_ref, gb_ref, o_ref, state_ref, *, tile_decay):
    @pl.when(pl.program_id(1) == 0)
    def _():
        state_ref[...] = jnp.zeros_like(state_ref)

    cos = cos_ref[...]
    sin = sin_ref[...]
    for h in range(RET_HEADS):
        qs = slice(h * RET_QK_DIM, (h + 1) * RET_QK_DIM)
        vs = slice(h * RET_V_DIM, (h + 1) * RET_V_DIM)
        q = _rotary(q_ref[:, qs].astype(F32), cos, sin) * (RET_QK_DIM ** -0.5)
        k = _rotary(k_ref[:, qs].astype(F32), cos, sin)
        v = v_ref[:, vs]
        scores = lax.dot_general(q.astype(BF16), k.astype(BF16),
                                 (((1,), (1,)), ((), ())),
                                 preferred_element_type=F32) * dmask_ref[h]
        o = jnp.dot(scores.astype(BF16), v, preferred_element_type=F32)
        state = state_ref[h]
        o = o + jnp.dot((q * qdec_ref[h]).astype(BF16), state.astype(BF16),
                        preferred_element_type=F32)
        kv = lax.dot_general((k * kdec_ref[h]).astype(BF16), v,
                             (((0,), (0,)), ((), ())),
                             preferred_element_type=F32)
        state_ref[h] = state * tile_decay[h] + kv
        mu = jnp.mean(o, axis=-1, keepdims=True)
        oc = o - mu
        var = jnp.mean(oc * oc, axis=-1, keepdims=True)
        on = oc * lax.rsqrt(var + EPS) * gg_ref[:, vs] + gb_ref[:, vs]
        g = g_ref[:, vs].astype(F32)
        o_ref[:, vs] = (g * jax.nn.sigmoid(g) * on).astype(BF16)


def _retention_consts(seq):
    t = RET_TILE
    pos = jnp.arange(seq, dtype=F32)
    inv_freq = ROPE_BASE ** (-jnp.arange(0, RET_QK_DIM, 2, dtype=F32) / RET_QK_DIM)
    ang = pos[:, None] * inv_freq[None, :]
    cos = jnp.cos(ang)
    sin = jnp.sin(ang)
    log_gamma = jnp.log(1.0 - 2.0 ** (-5.0 - jnp.arange(RET_HEADS, dtype=F32)))
    idx = jnp.arange(t, dtype=F32)
    diff = idx[:, None] - idx[None, :]
    chunk_i = jnp.arange(t)[:, None] // CHUNK
    chunk_j = jnp.arange(t)[None, :] // CHUNK
    dmask = jnp.where((chunk_j <= chunk_i)[None],
                      jnp.exp(jnp.abs(diff)[None] * log_gamma[:, None, None]), 0.0)
    qdec = jnp.exp((idx + 1.0)[None, :, None] * log_gamma[:, None, None])
    kdec = jnp.exp((t - 1.0 - idx)[None, :, None] * log_gamma[:, None, None])
    return cos, sin, dmask, qdec, kdec


def _tile_decay():
    gamma = 1.0 - 2.0 ** (-5.0 - np.arange(RET_HEADS, dtype=np.float64))
    return tuple(float(x) for x in gamma ** RET_TILE)


def _retention(u, gn_g, gn_b, bsz, seq):
    n_tok = u.shape[0]
    t = RET_TILE
    tpb = seq // t
    cos, sin, dmask, qdec, kdec = _retention_consts(seq)
    half = RET_QK_DIM // 2

    def col(width, off):
        return pl.BlockSpec((t, width), lambda b, i: (b * tpb + i, off // width))

    const3 = lambda shape: pl.BlockSpec(shape, lambda b, i: (0, 0, 0))
    vec = pl.BlockSpec((1, RET_V), lambda b, i: (0, 0))
    return pl.pallas_call(
        functools.partial(_ret_kernel, tile_decay=_tile_decay()),
        out_shape=jax.ShapeDtypeStruct((n_tok, RET_V), BF16),
        grid=(bsz, tpb),
        in_specs=[col(RET_QK, COL_Q), col(RET_QK, COL_K), col(RET_V, COL_V),
                  col(RET_V, COL_G),
                  pl.BlockSpec((t, half), lambda b, i: (i, 0)),
                  pl.BlockSpec((t, half), lambda b, i: (i, 0)),
                  const3((RET_HEADS, t, t)), const3((RET_HEADS, t, 1)),
                  const3((RET_HEADS, t, 1)), vec, vec],
        out_specs=pl.BlockSpec((t, RET_V), lambda b, i: (b * tpb + i, 0)),
        scratch_shapes=[pltpu.VMEM((RET_HEADS, RET_QK_DIM, RET_V_DIM), F32)],
        compiler_params=_params(("arbitrary", "arbitrary")),
        name="retention",
    )(u, u, u, u, cos, sin, dmask, qdec, kdec,
      gn_g.reshape(1, RET_V), gn_b.reshape(1, RET_V))


def _merge_kernel(x_ref, yc_ref, yr_ref, gc_ref, gr_ref, gt_ref, pw_ref, wo_ref,
                  wout_ref, o_ref, *, tn):
    d = o_ref.shape[1]
    yc_in = yc_ref[...]
    yr_in = yr_ref[...]
    acc = None
    for c0 in range(0, d, tn):
        cs = slice(c0, c0 + tn)
        yc = jnp.dot(yc_in, pw_ref[:, cs], preferred_element_type=F32)
        yr = jnp.dot(yr_in, wo_ref[:, cs], preferred_element_type=F32)
        m = (jax.nn.sigmoid(gc_ref[:, cs].astype(F32)) * yc
             + jax.nn.sigmoid(gr_ref[:, cs].astype(F32)) * yr)
        part = jnp.dot(m.astype(BF16), wout_ref[cs, :], preferred_element_type=F32)
        acc = part if acc is None else acc + part
    o_ref[...] = x_ref[...] + gt_ref[...] * acc


def _merge(x2, ycf, yrf, u, mod4, pw_w, w_o, w_out, seq, tm=256, tn=512):
    n_tok, d = x2.shape
    tpb = seq // tm
    row = pl.BlockSpec((tm, d), lambda i: (i, 0))
    resident = lambda shape: pl.BlockSpec(shape, lambda i: (0, 0),
                                          pipeline_mode=pl.Buffered(1))
    return pl.pallas_call(
        functools.partial(_merge_kernel, tn=tn),
        out_shape=jax.ShapeDtypeStruct((n_tok, d), F32),
        grid=(n_tok // tm,),
        in_specs=[row,
                  pl.BlockSpec((tm, CONV_DIM), lambda i: (i, 0)),
                  pl.BlockSpec((tm, RET_V), lambda i: (i, 0)),
                  pl.BlockSpec((tm, d), lambda i: (i, COL_GC // d)),
                  pl.BlockSpec((tm, d), lambda i: (i, COL_GR // d)),
                  pl.BlockSpec((None, None, 1, d), lambda i: (i // tpb, 5, 0, 0)),
                  resident((CONV_DIM, d)), resident((RET_V, d)), resident((d, d))],
        out_specs=row,
        compiler_params=_params(("arbitrary",)),
        name="merge",
    )(x2, ycf, yrf, u, u, mod4, pw_w, w_o, w_out)


def kernel(x, c, ada_w, ada_b, ffn1_norm, ffn1_w1, ffn1_w3, ffn1_w2, mix_norm, w_in,
           conv_dw_w, conv_dw_b, conv_ln_g, conv_ln_b, conv_pw_w, ret_gn_g, ret_gn_b,
           ret_w_o, w_out, ffn2_norm, ffn2_w1, ffn2_w3, ffn2_w2, final_norm,
           ada_f_w, ada_f_b):
    bsz, seq, d = x.shape
    depth = ada_w.shape[0]
    x2 = x.reshape(bsz * seq, d)
    f4 = _adaln(c, ada_f_w, ada_f_b).reshape(bsz, 2, 1, d)
    for l in range(depth):
        last = l == depth - 1
        mod4 = _adaln(c, ada_w[l], ada_b[l]).reshape(bsz, N_MOD, 1, d)
        x2 = _ffn(x2, ffn1_norm[l], mod4, 0, ffn1_w1[l].astype(BF16),
                  ffn1_w3[l].astype(BF16), ffn1_w2[l].astype(BF16), seq)
        u = _in_proj(x2, mix_norm[l], mod4, w_in[l].astype(BF16), seq)
        ycf = _conv_branch(u, conv_dw_w[l], conv_dw_b[l], conv_ln_g[l], conv_ln_b[l],
                           bsz, seq)
        yrf = _retention(u, ret_gn_g[l], ret_gn_b[l], bsz, seq)
        x2 = _merge(x2, ycf, yrf, u, mod4, conv_pw_w[l].astype(BF16),
                    ret_w_o[l].astype(BF16), w_out[l].astype(BF16), seq)
        x2 = _ffn(x2, ffn2_norm[l], mod4, 6, ffn2_w1[l].astype(BF16),
                  ffn2_w3[l].astype(BF16), ffn2_w2[l].astype(BF16), seq,
                  final=(final_norm, f4) if last else None)
    if depth == 0:
        raise NotImplementedError("kernel expects at least one layer")
    return x2.reshape(bsz, seq, d)
```
